```python
import math
import jax, jax.numpy as jnp
from jax import lax
import numpy as np

D_MODEL = 2048
BATCH = 4
SEQ = 2048
DEPTH = 4

N_MIXERS = 3
HEAD_DIM = 128
ROT_DIM = HEAD_DIM // 4
ROPE_THETA = 500000.0
A_HEADS = D_MODEL // (2 * HEAD_DIM)
A_WIDTH = A_HEADS * 2 * HEAD_DIM
FNET_GROUPS = 4
FNET_GROUP_WIDTH = D_MODEL // FNET_GROUPS
DILATED_CONFIGS = ((128, 1), (512, 4), (2048, 16))
C_GROUPS = len(DILATED_CONFIGS)
C_HEADS = D_MODEL // (2 * HEAD_DIM)
C_WIDTH = C_HEADS * HEAD_DIM
N_GROUPS = 8
EXPERTS_PER_GROUP = 8
N_EXPERTS = N_GROUPS * EXPERTS_PER_GROUP
TOP_K = 2
D_EXPERT = D_MODEL // 4
ATTN_Q_BLOCK = 128
DIL_Q_BLOCK = 64
MOE_BLOCK = 128
RMS_EPS = 1e-6
NEG_INF = -1e30

kernel_name = 'hybrid_diffattn_fnet_dilated_hmoe_encoder'


def rms_norm(x, g):
    xf = x.astype(jnp.float32)
    y = xf * lax.rsqrt(jnp.mean(xf * xf, axis=-1, keepdims=True) + RMS_EPS)
    return (y * g.astype(jnp.float32)).astype(x.dtype)


def lambda_init(layer):
    return 0.8 - 0.6 * math.exp(-0.3 * layer)


def rope_tables(positions):
    inv_freq = ROPE_THETA ** (-jnp.arange(0, ROT_DIM, 2, dtype=jnp.float32) / ROT_DIM)
    ang = positions.astype(jnp.float32)[..., None] * inv_freq
    return jnp.cos(ang), jnp.sin(ang)


def apply_rope(x, cos, sin):
    shape = cos.shape[:2] + (1,) * (x.ndim - 3) + cos.shape[-1:]
    c = cos.reshape(shape)
    s = sin.reshape(shape)
    half = ROT_DIM // 2
    xr = x[..., :ROT_DIM].astype(jnp.float32)
    x1, x2 = xr[..., :half], xr[..., half:]
    rot = jnp.concatenate([x1 * c - x2 * s, x2 * c + x1 * s], axis=-1).astype(x.dtype)
    return jnp.concatenate([rot, x[..., ROT_DIM:]], axis=-1)


def diff_attention(h, w_qkv, w_o, lam_p, subln_g, cos, sin, lam_init):
    b, s, _ = h.shape
    qkv = h @ w_qkv
    q = qkv[..., :A_WIDTH].reshape(b, s, A_HEADS, 2, HEAD_DIM)
    k = qkv[..., A_WIDTH:2 * A_WIDTH].reshape(b, s, A_HEADS, 2, HEAD_DIM)
    v = qkv[..., 2 * A_WIDTH:].reshape(b, s, A_HEADS, 2 * HEAD_DIM)
    q = apply_rope(q, cos, sin) * (HEAD_DIM ** -0.5)
    k = apply_rope(k, cos, sin)
    lam_p = lam_p.astype(jnp.float32)
    lam = jnp.exp(jnp.sum(lam_p[0] * lam_p[1])) - jnp.exp(jnp.sum(lam_p[2] * lam_p[3])) + lam_init

    def block(bi):
        qb = lax.dynamic_slice_in_dim(q, bi * ATTN_Q_BLOCK, ATTN_Q_BLOCK, axis=1)
        sc = jnp.einsum('bqhcd,bkhcd->bhcqk', qb, k).astype(jnp.float32)
        p = jax.nn.softmax(sc, axis=-1)
        a = p[:, :, 0] - lam * p[:, :, 1]
        return jnp.einsum('bhqk,bkhe->bqhe', a.astype(v.dtype), v)

    out = lax.map(block, jnp.arange(s // ATTN_Q_BLOCK))
    out = out.transpose(1, 0, 2, 3, 4).reshape(b, s, A_HEADS, 2 * HEAD_DIM)
    out = rms_norm(out, subln_g) * (1.0 - lam_init)
    return out.reshape(b, s, A_WIDTH) @ w_o


def fourier_mix(h, w_o):
    b, s, d = h.shape
    xf = h.astype(jnp.float32).reshape(b, s, FNET_GROUPS, FNET_GROUP_WIDTH)
    y = jnp.fft.fft2(xf, axes=(1, 3), norm='ortho').real
    return y.astype(h.dtype).reshape(b, s, d) @ w_o


def dilated_attention(h, w_qkv, w_o, cos, sin):
    b, s, _ = h.shape
    qkv = (h @ w_qkv).reshape(b, s, 3, C_GROUPS, C_HEADS, HEAD_DIM)
    q = apply_rope(qkv[:, :, 0], cos, sin) * (HEAD_DIM ** -0.5)
    k = apply_rope(qkv[:, :, 1], cos, sin)
    v = qkv[:, :, 2]
    k_groups = [k[:, :, g] for g in range(C_GROUPS)]
    v_groups = [v[:, :, g] for g in range(C_GROUPS)]

    def block(bi):
        q0 = bi * DIL_Q_BLOCK
        qb = lax.dynamic_slice_in_dim(q, q0, DIL_Q_BLOCK, axis=1)
        qpos = q0 + jnp.arange(DIL_Q_BLOCK, dtype=jnp.int32)
        outs, lses = [], []
        for g, (win, dil) in enumerate(DILATED_CONFIGS):
            half = win // (2 * dil)
            offs = dil * jnp.arange(-half, half + 1, dtype=jnp.int32)
            idx = qpos[:, None] + offs[None, :]
            valid = (idx >= 0) & (idx < s)
            idx = jnp.clip(idx, 0, s - 1)
            kg = jnp.take(k_groups[g], idx, axis=1)
            vg = jnp.take(v_groups[g], idx, axis=1)
            sc = jnp.einsum('bqhd,bqjhd->bqhj', qb[:, :, g], kg).astype(jnp.float32)
            sc = jnp.where(valid[None, :, None, :], sc, NEG_INF)
            lse = jax.nn.logsumexp(sc, axis=-1)
            p = jnp.exp(sc - lse[..., None])
            outs.append(jnp.einsum('bqhj,bqjhd->bqhd', p.astype(vg.dtype), vg))
            lses.append(lse)
        alpha = jax.nn.softmax(jnp.stack(lses, axis=0), axis=0)
        o_all = jnp.stack(outs, axis=0)
        return jnp.sum(alpha[..., None].astype(o_all.dtype) * o_all, axis=0)

    out = lax.map(block, jnp.arange(s // DIL_Q_BLOCK))
    out = out.transpose(1, 0, 2, 3, 4).reshape(b, s, C_WIDTH)
    return out @ w_o


def hierarchical_moe(h, w_rg, w_re, w_gate, w_up, w_down):
    b, s, d = h.shape
    t = b * s
    xt = h.reshape(t, d)
    grp_logits = (xt @ w_rg).astype(jnp.float32)
    grp_prob = jax.nn.softmax(grp_logits, axis=-1)
    _, g_idx = lax.top_k(grp_logits, 1)
    g_sel = g_idx[:, 0]
    exp_logits = (xt @ w_re).astype(jnp.float32).reshape(t, N_GROUPS, EXPERTS_PER_GROUP)
    in_grp = jnp.take_along_axis(exp_logits, g_sel[:, None, None], axis=1)[:, 0]
    top_val, top_idx = lax.top_k(in_grp, TOP_K)
    gate = jax.nn.softmax(top_val, axis=-1) * jnp.take_along_axis(grp_prob, g_sel[:, None], axis=1)
    expert_id = g_sel[:, None] * EXPERTS_PER_GROUP + top_idx

    n_assign = t * TOP_K
    e_flat = expert_id.reshape(n_assign)
    t_flat = jnp.repeat(jnp.arange(t, dtype=jnp.int32), TOP_K)
    w_flat = gate.reshape(n_assign)
    order = jnp.argsort(e_flat)
    e_s, t_s, w_s = e_flat[order], t_flat[order], w_flat[order]
    counts = jnp.bincount(e_flat, length=N_EXPERTS)
    starts = jnp.cumsum(counts) - counts
    padded = ((counts + MOE_BLOCK - 1) // MOE_BLOCK) * MOE_BLOCK
    pend = jnp.cumsum(padded)
    pstart = pend - padded
    dest = pstart[e_s] + (jnp.arange(n_assign) - starts[e_s])
    n_blocks = -(-n_assign // MOE_BLOCK) + N_EXPERTS
    n_slots = n_blocks * MOE_BLOCK
    slot_tok = jnp.full((n_slots,), t, dtype=jnp.int32).at[dest].set(t_s)
    slot_w = jnp.zeros((n_slots,), jnp.float32).at[dest].set(w_s)
    block_exp = jnp.minimum(jnp.searchsorted(pend, jnp.arange(n_blocks) * MOE_BLOCK, side='right'),
                            N_EXPERTS - 1)
    x_pad = jnp.concatenate([xt, jnp.zeros((1, d), xt.dtype)], axis=0)
    xs = x_pad[slot_tok].reshape(n_blocks, MOE_BLOCK, d)

    def expert_block(args):
        xb, e = args
        hid = jax.nn.silu(xb @ w_gate[e]) * (xb @ w_up[e])
        return hid @ w_down[e]

    ys = lax.map(expert_block, (xs, block_exp)).reshape(n_slots, d)
    ys = ys * slot_w[:, None].astype(ys.dtype)
    out = jax.ops.segment_sum(ys, slot_tok, num_segments=t + 1)[:t]
    return out.reshape(b, s, d)


def setup_inputs(seed: int = 0) -> dict:
    key = jax.random.key(seed)
    ks = jax.random.split(key, 20)
    n_a = (DEPTH + 2) // 3
    n_b = (DEPTH + 1) // 3
    n_c = DEPTH // 3

    def nrm(k, shape, fan_in):
        return jax.random.normal(k, shape, jnp.float32) * (fan_in ** -0.5)

    def gain(k, shape):
        return 1.0 + 0.02 * jax.random.normal(k, shape, jnp.float32)

    x = jax.random.normal(ks[0], (BATCH, SEQ, D_MODEL), jnp.float32)
    offset = jax.random.randint(ks[1], (BATCH, 1), 0, 4096, dtype=jnp.int32)
    positions = (jnp.arange(SEQ, dtype=jnp.int32)[None, :] + offset).astype(jnp.int32)
    return {
        'x': x,
        'positions': positions,
        'norm_mix': gain(ks[2], (DEPTH, D_MODEL)),
        'norm_ffn': gain(ks[3], (DEPTH, D_MODEL)),
        'norm_final': gain(ks[4], (D_MODEL,)),
        'a_w_qkv': nrm(ks[5], (n_a, D_MODEL, 3 * A_WIDTH), D_MODEL),
        'a_w_o': nrm(ks[6], (n_a, A_WIDTH, D_MODEL), A_WIDTH),
        'a_lambda': 0.1 * jax.random.normal(ks[7], (n_a, 4, HEAD_DIM), jnp.float32),
        'a_subln': gain(ks[8], (n_a, 2 * HEAD_DIM)),
        'b_w_o': nrm(ks[9], (n_b, D_MODEL, D_MODEL), D_MODEL),
        'c_w_qkv': nrm(ks[10], (n_c, D_MODEL, 3 * C_GROUPS * C_WIDTH), D_MODEL),
        'c_w_o': nrm(ks[11], (n_c, C_WIDTH, D_MODEL), C_WIDTH),
        'router_group': nrm(ks[12], (DEPTH, D_MODEL, N_GROUPS), D_MODEL),
        'router_expert': nrm(ks[13], (DEPTH, D_MODEL, N_EXPERTS), D_MODEL),
        'w_gate': nrm(ks[14], (DEPTH, N_EXPERTS, D_MODEL, D_EXPERT), D_MODEL),
        'w_up': nrm(ks[15], (DEPTH, N_EXPERTS, D_MODEL, D_EXPERT), D_MODEL),
        'w_down': nrm(ks[16], (DEPTH, N_EXPERTS, D_EXPERT, D_MODEL), D_EXPERT),
    }


def reference(x, positions, norm_mix, norm_ffn, norm_final, a_w_qkv, a_w_o, a_lambda, a_subln,
              b_w_o, c_w_qkv, c_w_o, router_group, router_expert, w_gate, w_up, w_down):
    cos, sin = rope_tables(positions)
    for i in range(DEPTH):
        kind, j = i % N_MIXERS, i // N_MIXERS
        h = rms_norm(x, norm_mix[i])
        if kind == 0:
            mix = diff_attention(h, a_w_qkv[j], a_w_o[j], a_lambda[j], a_subln[j], cos, sin, lambda_init(i))
        elif kind == 1:
            mix = fourier_mix(h, b_w_o[j])
        else:
            mix = dilated_attention(h, c_w_qkv[j], c_w_o[j], cos, sin)
        x = x + mix
        x = x + hierarchical_moe(rms_norm(x, norm_ffn[i]), router_group[i], router_expert[i],
                                 w_gate[i], w_up[i], w_down[i])
    return rms_norm(x, norm_final)
```

```python
import functools
import math

import numpy as np
import jax
import jax.numpy as jnp
from jax import lax
from jax.experimental import pallas as pl
from jax.experimental.pallas import tpu as pltpu

D_MODEL = 2048
BATCH = 4
SEQ = 2048
DEPTH = 4
TOKENS = BATCH * SEQ
N_MIXERS = 3
HEAD_DIM = 128
ROT_DIM = HEAD_DIM // 4
ROPE_THETA = 500000.0
A_HEADS = D_MODEL // (2 * HEAD_DIM)
A_WIDTH = A_HEADS * 2 * HEAD_DIM
FNET_GROUPS = 4
FNET_GROUP_WIDTH = D_MODEL // FNET_GROUPS
DILATIONS = (1, 4, 16)
DIL_HALF = 64
C_GROUPS = len(DILATIONS)
C_HEADS = D_MODEL // (2 * HEAD_DIM)
C_WIDTH = C_HEADS * HEAD_DIM
N_GROUPS = 8
EXPERTS_PER_GROUP = 8
N_EXPERTS = N_GROUPS * EXPERTS_PER_GROUP
TOP_K = 2
D_EXPERT = D_MODEL // 4
MOE_BLOCK = 128
MOE_BLOCKS = (TOKENS * TOP_K) // MOE_BLOCK + N_EXPERTS
MOE_SLOTS = MOE_BLOCKS * MOE_BLOCK
RMS_EPS = 1e-6
NEG_INF = -1e30
LANES = 128
VMEM_LIMIT = 52 * 1024 * 1024

F32 = jnp.float32
BF16 = jnp.bfloat16


def _params(*semantics):
    return pltpu.CompilerParams(dimension_semantics=semantics, vmem_limit_bytes=VMEM_LIMIT)


def _norm_kernel(x_ref, g_ref, h_ref):
    x = x_ref[...]
    ms = jnp.mean(x * x, axis=-1, keepdims=True)
    h_ref[...] = ((x * lax.rsqrt(ms + RMS_EPS)) * g_ref[...]).astype(h_ref.dtype)


def _rms_norm(x, gains, layer, out_dtype, tm=256):
    g3 = gains.reshape(gains.shape[0], 1, D_MODEL)
    return pl.pallas_call(
        _norm_kernel,
        grid=(TOKENS // tm,),
        in_specs=[pl.BlockSpec((tm, D_MODEL), lambda i: (i, 0)),
                  pl.BlockSpec((None, 1, D_MODEL), lambda i: (layer, 0, 0))],
        out_specs=pl.BlockSpec((tm, D_MODEL), lambda i: (i, 0)),
        out_shape=jax.ShapeDtypeStruct((TOKENS, D_MODEL), out_dtype),
        compiler_params=_params("arbitrary"),
        name="rms_norm",
    )(x, g3)


def _combine_kernel(x_ref, y0_ref, y1_ref, gate_ref, g_ref, *out_refs, emit_sum):
    gates = gate_ref[...]
    x = x_ref[...] + (gates[:, 0:1] * y0_ref[...] + gates[:, 1:2] * y1_ref[...])
    if emit_sum:
        out_refs[0][...] = x
    h_ref = out_refs[-1]
    ms = jnp.mean(x * x, axis=-1, keepdims=True)
    h_ref[...] = ((x * lax.rsqrt(ms + RMS_EPS)) * g_ref[...]).astype(h_ref.dtype)


def _moe_combine_norm(x, y, gates, gains, layer, out_dtype, emit_sum, tm=256):
    g3 = gains.reshape(-1, 1, D_MODEL)
    nt = TOKENS // tm
    row = pl.BlockSpec((tm, D_MODEL), lambda i: (i, 0))
    out_shape = [jax.ShapeDtypeStruct((TOKENS, D_MODEL), out_dtype)]
    out_specs = [row]
    if emit_sum:
        out_shape.insert(0, jax.ShapeDtypeStruct((TOKENS, D_MODEL), F32))
        out_specs.insert(0, row)
    return pl.pallas_call(
        functools.partial(_combine_kernel, emit_sum=emit_sum),
        grid=(nt,),
        in_specs=[row, row,
                  pl.BlockSpec((tm, D_MODEL), lambda i: (i + nt, 0)),
                  pl.BlockSpec((tm, LANES), lambda i: (i, 0)),
                  pl.BlockSpec((None, 1, D_MODEL), lambda i: (layer, 0, 0))],
        out_specs=out_specs,
        out_shape=out_shape,
        compiler_params=_params("arbitrary"),
        name="moe_combine_norm",
    )(x, y, y, gates, g3)


def _mm_kernel(*refs, rope, residual, tn):
    a_ref, w_ref = refs[0], refs[1]
    pos = 2
    if rope:
        c_ref, s1_ref, s2_ref = refs[2:5]
        pos = 5
    if residual:
        r_ref = refs[pos]
        pos += 1
    o_ref, wbf_ref = refs[pos], refs[pos + 1]

    @pl.when(pl.program_id(1) == 0)
    def _():
        wbf_ref[...] = w_ref[...].astype(BF16)

    acc = jnp.dot(a_ref[...], wbf_ref[...], preferred_element_type=F32)
    if rope:
        reps = tn // LANES
        c = jnp.tile(c_ref[...], (1, reps))
        s1 = jnp.tile(s1_ref[...], (1, reps))
        s2 = jnp.tile(s2_ref[...], (1, reps))
        half = ROT_DIM // 2
        acc = acc * c + pltpu.roll(acc, tn - half, 1) * s1 + pltpu.roll(acc, half, 1) * s2
    if residual:
        acc = r_ref[...] + acc
    o_ref[...] = acc.astype(o_ref.dtype)


def _matmul(a, w, layer, *, n_out, out_dtype, tm=512, tn=512, residual=None, rope=None):
    m, k = a.shape
    nj, ni = n_out // tn, m // tm
    in_specs = [pl.BlockSpec((tm, k), lambda j, i: (i, 0)),
                pl.BlockSpec((None, k, tn), lambda j, i: (layer, 0, j))]
    args = [a, w]
    if rope is not None:
        per_part = nj // rope[0].shape[0]
        tab = pl.BlockSpec((None, tm, LANES), lambda j, i: (j // per_part, i, 0))
        in_specs += [tab, tab, tab]
        args += list(rope)
    if residual is not None:
        in_specs.append(pl.BlockSpec((tm, tn), lambda j, i: (i, j)))
        args.append(residual)
    return pl.pallas_call(
        functools.partial(_mm_kernel, rope=rope is not None, residual=residual is not None, tn=tn),
        grid=(nj, ni),
        in_specs=in_specs,
        out_specs=pl.BlockSpec((tm, tn), lambda j, i: (i, j)),
        out_shape=jax.ShapeDtypeStruct((m, n_out), out_dtype),
        scratch_shapes=[pltpu.VMEM((k, tn), BF16)],
        compiler_params=_params("arbitrary", "arbitrary"),
        name="matmul",
    )(*args)


def _rope_tables(positions, parts):
    inv_freq = ROPE_THETA ** (-jnp.arange(0, ROT_DIM, 2, dtype=F32) / ROT_DIM)
    ang = positions.astype(F32)[..., None] * inv_freq
    cos = jnp.cos(ang).reshape(TOKENS, ROT_DIM // 2)
    sin = jnp.sin(ang).reshape(TOKENS, ROT_DIM // 2)
    half = ROT_DIM // 2
    ones = jnp.ones((TOKENS, LANES - ROT_DIM), F32)
    c_full = jnp.concatenate([cos, cos, ones], axis=1)
    s_lo = jnp.concatenate([-sin, jnp.zeros((TOKENS, LANES - half), F32)], axis=1)
    s_hi = jnp.concatenate([jnp.zeros((TOKENS, half), F32), sin,
                            jnp.zeros((TOKENS, LANES - ROT_DIM), F32)], axis=1)
    cs, lo, hi = [], [], []
    for scale in parts:
        if scale is None:
            cs.append(jnp.ones((TOKENS, LANES), F32))
            lo.append(jnp.zeros((TOKENS, LANES), F32))
            hi.append(jnp.zeros((TOKENS, LANES), F32))
        else:
            cs.append(c_full * scale)
            lo.append(s_lo * scale)
            hi.append(s_hi * scale)
    return jnp.stack(cs), jnp.stack(lo), jnp.stack(hi)


def _softmax_pv(q, k, v):
    s = lax.dot_general(q, k, (((1,), (1,)), ((), ())), preferred_element_type=F32)
    m = jnp.max(s, axis=-1, keepdims=True)
    e = jnp.exp(s - m)
    l = jnp.sum(e, axis=-1, keepdims=True)
    return jnp.dot(e.astype(BF16), v, preferred_element_type=F32) / l


def _diff_attn_kernel(q_ref, k_ref, v_ref, lam_ref, g_ref, o_ref, *, lam_init):
    lp = lam_ref[...]
    lam = (jnp.exp(jnp.sum(lp[0:1] * lp[1:2], axis=-1, keepdims=True))
           - jnp.exp(jnp.sum(lp[2:3] * lp[3:4], axis=-1, keepdims=True)) + lam_init)
    v = v_ref[...]
    o0 = _softmax_pv(q_ref[:, :HEAD_DIM], k_ref[:, :HEAD_DIM], v)
    o1 = _softmax_pv(q_ref[:, HEAD_DIM:], k_ref[:, HEAD_DIM:], v)
    out = o0 - lam * o1
    ms = jnp.mean(out * out, axis=-1, keepdims=True)
    y = (out * lax.rsqrt(ms + RMS_EPS)) * g_ref[...]
    o_ref[...] = (y * (1.0 - lam_init)).astype(o_ref.dtype)


def _diff_attention(qkv, lam_p, subln, j, lam_init, tq=256):
    nq = SEQ // tq
    hw = 2 * HEAD_DIM
    g3 = subln.reshape(-1, 1, hw)
    return pl.pallas_call(
        functools.partial(_diff_attn_kernel, lam_init=lam_init),
        grid=(BATCH, A_HEADS, nq),
        in_specs=[pl.BlockSpec((tq, hw), lambda b, h, t: (b * nq + t, h)),
                  pl.BlockSpec((SEQ, hw), lambda b, h, t: (b, A_HEADS + h)),
                  pl.BlockSpec((SEQ, hw), lambda b, h, t: (b, 2 * A_HEADS + h)),
                  pl.BlockSpec((None, 4, HEAD_DIM), lambda b, h, t: (j, 0, 0)),
                  pl.BlockSpec((None, 1, hw), lambda b, h, t: (j, 0, 0))],
        out_specs=pl.BlockSpec((tq, hw), lambda b, h, t: (b * nq + t, h)),
        out_shape=jax.ShapeDtypeStruct((TOKENS, A_WIDTH), BF16),
        compiler_params=_params("arbitrary", "arbitrary", "arbitrary"),
        name="diff_attention",
    )(qkv, qkv, qkv, lam_p, g3)


def _dft_tables():
    gw = FNET_GROUP_WIDTH
    ang_c = 2.0 * np.pi * (np.outer(np.arange(gw), np.arange(gw)) % gw) / gw
    scale = 1.0 / math.sqrt(SEQ * gw)
    chan = np.concatenate([np.cos(ang_c), np.sin(ang_c)], axis=1) * scale
    ang_s = 2.0 * np.pi * (np.outer(np.arange(SEQ), np.arange(SEQ)) % SEQ) / SEQ
    return (chan.astype(np.float32)[None], np.cos(ang_s).astype(np.float32),
            np.sin(ang_s).astype(np.float32))


def _seq_dft_kernel(cs_ref, ss_ref, u_ref, o_ref):
    gw = FNET_GROUP_WIDTH
    acc = (jnp.dot(cs_ref[...], u_ref[:, :gw], preferred_element_type=F32)
           - jnp.dot(ss_ref[...], u_ref[:, gw:], preferred_element_type=F32))
    o_ref[...] = acc.astype(o_ref.dtype)


def _fourier_mix(h, tr=512):
    gw = FNET_GROUP_WIDTH
    chan, cos_s, sin_s = _dft_tables()
    u = _matmul(h.reshape(TOKENS * FNET_GROUPS, gw), jnp.asarray(chan), 0,
                n_out=2 * gw, out_dtype=BF16, tm=1024, tn=2 * gw)
    u = u.reshape(TOKENS, FNET_GROUPS * 2 * gw)
    nt = SEQ // tr
    return pl.pallas_call(
        _seq_dft_kernel,
        grid=(BATCH, FNET_GROUPS, nt),
        in_specs=[pl.BlockSpec((tr, SEQ), lambda b, g, t: (t, 0)),
                  pl.BlockSpec((tr, SEQ), lambda b, g, t: (t, 0)),
                  pl.BlockSpec((SEQ, 2 * gw), lambda b, g, t: (b, g))],
        out_specs=pl.BlockSpec((tr, gw), lambda b, g, t: (b * nt + t, g)),
        out_shape=jax.ShapeDtypeStruct((TOKENS, D_MODEL), BF16),
        compiler_params=_params("arbitrary", "arbitrary", "arbitrary"),
        name="seq_dft",
    )(jnp.asarray(cos_s).astype(BF16), jnp.asarray(sin_s).astype(BF16), u)


def _dil_attn_kernel(q_ref, k_ref, v_ref, o_ref, lse_ref, *, seq_len, tq, win):
    m0 = pl.program_id(2) * tq
    start = pl.multiple_of(jnp.clip(m0 - (win - tq) // 2, 0, seq_len - win), tq)
    qpos = m0 + lax.broadcasted_iota(jnp.int32, (tq, win), 0)
    kpos = start + lax.broadcasted_iota(jnp.int32, (tq, win), 1)
    valid = jnp.abs(kpos - qpos) <= DIL_HALF
    lane = lax.broadcasted_iota(jnp.int32, (tq, LANES), 1)
    lse_all = jnp.zeros((tq, LANES), F32)
    for h in range(C_HEADS):
        cols = slice(h * HEAD_DIM, (h + 1) * HEAD_DIM)
        s = lax.dot_general(q_ref[:, cols], k_ref[pl.ds(start, win), cols],
                            (((1,), (1,)), ((), ())), preferred_element_type=F32)
        s = jnp.where(valid, s, NEG_INF)
        m = jnp.max(s, axis=-1, keepdims=True)
        e = jnp.exp(s - m)
        l = jnp.sum(e, axis=-1, keepdims=True)
        o = jnp.dot(e.astype(BF16), v_ref[pl.ds(start, win), cols], preferred_element_type=F32)
        o_ref[:, cols] = o / l
        lse_all = jnp.where(lane == h, m + jnp.log(l), lse_all)
    lse_ref[...] = lse_all


def _dilated_group(qkv, g, tq=128):
    dil = DILATIONS[g]
    seq_len = SEQ // dil
    win = min(3 * tq, seq_len)
    nq = seq_len // tq
    nblk = 3 * C_GROUPS
    view = qkv.reshape(BATCH * seq_len, dil * nblk * C_WIDTH)
    o, lse = pl.pallas_call(
        functools.partial(_dil_attn_kernel, seq_len=seq_len, tq=tq, win=win),
        grid=(BATCH, dil, nq),
        in_specs=[pl.BlockSpec((tq, C_WIDTH), lambda b, r, t: (b * nq + t, r * nblk + g)),
                  pl.BlockSpec((seq_len, C_WIDTH), lambda b, r, t: (b, r * nblk + C_GROUPS + g)),
                  pl.BlockSpec((seq_len, C_WIDTH), lambda b, r, t: (b, r * nblk + 2 * C_GROUPS + g))],
        out_specs=[pl.BlockSpec((tq, C_WIDTH), lambda b, r, t: (b * nq + t, r)),
                   pl.BlockSpec((tq, LANES), lambda b, r, t: (b * nq + t, r))],
        out_shape=[jax.ShapeDtypeStruct((BATCH * seq_len, dil * C_WIDTH), F32),
                   jax.ShapeDtypeStruct((BATCH * seq_len, dil * LANES), F32)],
        compiler_params=_params("arbitrary", "arbitrary", "arbitrary"),
        name="dilated_attention",
    )(view, view, view)
    return o.reshape(TOKENS, C_WIDTH), lse.reshape(TOKENS, LANES)


def _merge_kernel(o0_ref, o1_ref, o2_ref, l0_ref, l1_ref, l2_ref, out_ref):
    l0, l1, l2 = l0_ref[...], l1_ref[...], l2_ref[...]
    mx = jnp.maximum(jnp.maximum(l0, l1), l2)
    w0, w1, w2 = jnp.exp(l0 - mx), jnp.exp(l1 - mx), jnp.exp(l2 - mx)
    den = w0 + w1 + w2
    a0, a1, a2 = w0 / den, w1 / den, w2 / den
    for h in range(C_HEADS):
        cols = slice(h * HEAD_DIM, (h + 1) * HEAD_DIM)
        merged = (a0[:, h:h + 1] * o0_ref[:, cols] + a1[:, h:h + 1] * o1_ref[:, cols]
                  + a2[:, h:h + 1] * o2_ref[:, cols])
        out_ref[:, cols] = merged.astype(out_ref.dtype)


def _merge_groups(outs, lses, tm=256):
    row = pl.BlockSpec((tm, C_WIDTH), lambda i: (i, 0))
    lrow = pl.BlockSpec((tm, LANES), lambda i: (i, 0))
    return pl.pallas_call(
        _merge_kernel,
        grid=(TOKENS // tm,),
        in_specs=[row, row, row, lrow, lrow, lrow],
        out_specs=row,
        out_shape=jax.ShapeDtypeStruct((TOKENS, C_WIDTH), BF16),
        compiler_params=_params("arbitrary"),
        name="merge_dilation_groups",
    )(*outs, *lses)


def _router_kernel(h_ref, wr_ref, ids_ref, gates_ref, cnt_ref, carry_ref, *, tm):
    @pl.when(pl.program_id(0) == 0)
    def _():
        carry_ref[...] = jnp.zeros_like(carry_ref)

    logits = jnp.dot(h_ref[...], wr_ref[...], preferred_element_type=F32,
                     precision=lax.Precision.HIGHEST)
    lane = lax.broadcasted_iota(jnp.int32, (tm, LANES), 1).astype(F32)
    neg = jnp.float32(-jnp.inf)
    big = jnp.float32(4 * LANES)
    gl = jnp.where(lane < N_GROUPS, logits, neg)
    gmax = jnp.max(gl, axis=-1, keepdims=True)
    g_sel = jnp.min(jnp.where(gl == gmax, lane, big), axis=-1, keepdims=True)
    g_prob = 1.0 / jnp.sum(jnp.exp(gl - gmax), axis=-1, keepdims=True)
    lo = N_GROUPS + g_sel * EXPERTS_PER_GROUP
    el = jnp.where((lane >= lo) & (lane < lo + EXPERTS_PER_GROUP), logits, neg)
    v1 = jnp.max(el, axis=-1, keepdims=True)
    i1 = jnp.min(jnp.where(el == v1, lane, big), axis=-1, keepdims=True)
    el2 = jnp.where(lane == i1, neg, el)
    v2 = jnp.max(el2, axis=-1, keepdims=True)
    i2 = jnp.min(jnp.where(el2 == v2, lane, big), axis=-1, keepdims=True)
    t = jnp.exp(v2 - v1)
    gate1 = g_prob * (1.0 / (1.0 + t))
    gate2 = g_prob * (t / (1.0 + t))
    oh1 = lane == i1
    oh2 = lane == i2
    both = jnp.where(oh1 | oh2, 1.0, 0.0)
    row = lax.broadcasted_iota(jnp.int32, (tm, tm), 0)
    col = lax.broadcasted_iota(jnp.int32, (tm, tm), 1)
    lower = jnp.where(col < row, 1.0, 0.0).astype(BF16)
    carry = carry_ref[0:1, :]
    cum = jnp.dot(lower, both.astype(BF16), preferred_element_type=F32) + carry
    r1 = jnp.sum(jnp.where(oh1, cum, 0.0), axis=-1, keepdims=True)
    r2 = jnp.sum(jnp.where(oh2, cum, 0.0), axis=-1, keepdims=True)
    total = carry + jnp.sum(both, axis=0, keepdims=True)
    carry_ref[...] = jnp.broadcast_to(total, carry_ref.shape)
    cnt_ref[...] = jnp.broadcast_to(total, cnt_ref.shape)
    ids = jnp.where(lane == 0, i1 - N_GROUPS,
                    jnp.where(lane == 1, i2 - N_GROUPS,
                              jnp.where(lane == 2, r1, jnp.where(lane == 3, r2, 0.0))))
    ids_ref[...] = ids.astype(jnp.int32)
    gates_ref[...] = jnp.where(lane == 0, gate1, jnp.where(lane == 1, gate2, 0.0))


def _router(hf, w_rg, w_re, layer, tm=512):
    wr = jnp.concatenate([w_rg[layer], w_re[layer],
                          jnp.zeros((D_MODEL, LANES - N_GROUPS - N_EXPERTS), F32)], axis=1)
    row = pl.BlockSpec((tm, LANES), lambda i: (i, 0))
    return pl.pallas_call(
        functools.partial(_router_kernel, tm=tm),
        grid=(TOKENS // tm,),
        in_specs=[pl.BlockSpec((tm, D_MODEL), lambda i: (i, 0)),
                  pl.BlockSpec((D_MODEL, LANES), lambda i: (0, 0))],
        out_specs=[row, row, pl.BlockSpec((8, LANES), lambda i: (0, 0))],
        out_shape=[jax.ShapeDtypeStruct((TOKENS, LANES), jnp.int32),
                   jax.ShapeDtypeStruct((TOKENS, LANES), F32),
                   jax.ShapeDtypeStruct((8, LANES), F32)],
        scratch_shapes=[pltpu.VMEM((8, LANES), F32)],
        compiler_params=_params("arbitrary"),
        name="moe_router",
    )(hf, wr)


def _expert_kernel(be_ref, nu_ref, dst_ref, h_hbm, wg_ref, wu_ref, wd_ref, y_hbm,
                   xbuf, ybuf, wg_bf, wu_bf, wd_bf, gsem, ssem):
    i = pl.program_id(0)
    n_used = nu_ref[0]
    slot = lax.rem(i, 2)

    def gather_copy(blk, r, sl):
        tok = dst_ref[blk * MOE_BLOCK + r] & (TOKENS - 1)
        return pltpu.make_async_copy(h_hbm.at[pl.ds(tok, 1)], xbuf.at[sl, pl.ds(r, 1)], gsem.at[sl])

    def scatter_copy(blk, r):
        dst = dst_ref[blk * MOE_BLOCK + r]
        return pltpu.make_async_copy(ybuf.at[pl.ds(r, 1)], y_hbm.at[pl.ds(dst, 1)], ssem.at[0])

    def for_rows(fn):
        def body(r, carry):
            fn(r)
            return carry
        lax.fori_loop(0, MOE_BLOCK, body, 0)

    @pl.when(i == 0)
    def _():
        ybuf[...] = jnp.zeros_like(ybuf)
        spare = pltpu.make_async_copy(ybuf, y_hbm.at[pl.ds(TOP_K * TOKENS, MOE_BLOCK)], ssem.at[0])
        spare.start()
        spare.wait()

    @pl.when((i == 0) & (n_used > 0))
    def _():
        for_rows(lambda r: gather_copy(0, r, 0).start())

    @pl.when(i + 1 < n_used)
    def _():
        for_rows(lambda r: gather_copy(i + 1, r, 1 - slot).start())

    @pl.when((i < n_used) & ((i == 0) | (be_ref[i] != be_ref[jnp.maximum(i - 1, 0)])))
    def _():
        wg_bf[...] = wg_ref[...].astype(BF16)
        wu_bf[...] = wu_ref[...].astype(BF16)
        wd_bf[...] = wd_ref[...].astype(BF16)

    @pl.when((i > 0) & (i <= n_used))
    def _():
        for_rows(lambda r: scatter_copy(i - 1, r).wait())

    @pl.when(i < n_used)
    def _():
        for_rows(lambda r: gather_copy(i, r, slot).wait())
        x = xbuf[slot].astype(BF16)
        gate = jnp.dot(x, wg_bf[...], preferred_element_type=F32)
        up = jnp.dot(x, wu_bf[...], preferred_element_type=F32)
        hid = (gate * jax.nn.sigmoid(gate)) * up
        ybuf[...] = jnp.dot(hid.astype(BF16), wd_bf[...], preferred_element_type=F32)
        for_rows(lambda r: scatter_copy(i, r).start())

    @pl.when((i == MOE_BLOCKS - 1) & (i < n_used))
    def _():
        for_rows(lambda r: scatter_copy(i, r).wait())


def _experts(hf, block_exp, n_used, slot_dst, w_gate, w_up, w_down, layer):
    grid_spec = pltpu.PrefetchScalarGridSpec(
        num_scalar_prefetch=3,
        grid=(MOE_BLOCKS,),
        in_specs=[pl.BlockSpec(memory_space=pl.ANY),
                  pl.BlockSpec((None, None, D_MODEL, D_EXPERT), lambda i, be, nu, sd: (layer, be[i], 0, 0)),
                  pl.BlockSpec((None, None, D_MODEL, D_EXPERT), lambda i, be, nu, sd: (layer, be[i], 0, 0)),
                  pl.BlockSpec((None, None, D_EXPERT, D_MODEL), lambda i, be, nu, sd: (layer, be[i], 0, 0))],
        out_specs=pl.BlockSpec(memory_space=pl.ANY),
        scratch_shapes=[pltpu.VMEM((2, MOE_BLOCK, D_MODEL), F32),
                        pltpu.VMEM((MOE_BLOCK, D_MODEL), F32),
                        pltpu.VMEM((D_MODEL, D_EXPERT), BF16),
                        pltpu.VMEM((D_MODEL, D_EXPERT), BF16),
                        pltpu.VMEM((D_EXPERT, D_MODEL), BF16),
                        pltpu.SemaphoreType.DMA((2,)),
                        pltpu.SemaphoreType.DMA((1,))])
    return pl.pallas_call(
        _expert_kernel,
        grid_spec=grid_spec,
        out_shape=jax.ShapeDtypeStruct((TOP_K * TOKENS + MOE_BLOCK, D_MODEL), F32),
        compiler_params=_params("arbitrary"),
        name="moe_experts",
    )(block_exp, n_used, slot_dst, hf, w_gate, w_up, w_down)


def _moe_plan(ids, counts):
    expert = ids[:, 0:TOP_K]
    rank = ids[:, TOP_K:2 * TOP_K]
    cnt = counts[0, N_GROUPS:N_GROUPS + N_EXPERTS].astype(jnp.int32)
    padded = ((cnt + MOE_BLOCK - 1) // MOE_BLOCK) * MOE_BLOCK
    pend = jnp.cumsum(padded)
    pstart = pend - padded
    dest = pstart[expert] + rank
    n_used = pend[-1] // MOE_BLOCK
    blk = jnp.arange(MOE_BLOCKS, dtype=jnp.int32)
    block_exp = jnp.minimum(jnp.searchsorted(pend, blk * MOE_BLOCK, side='right'), N_EXPERTS - 1)
    last = block_exp[jnp.maximum(n_used - 1, 0)]
    block_exp = jnp.where(blk < n_used, block_exp, last).astype(jnp.int32)
    spare = TOP_K * TOKENS + (jnp.arange(MOE_SLOTS, dtype=jnp.int32) % MOE_BLOCK)
    slot_dst = spare.at[dest.T.reshape(-1)].set(jnp.arange(TOP_K * TOKENS, dtype=jnp.int32))
    return block_exp, n_used.reshape(1).astype(jnp.int32), slot_dst


def _lambda_init(layer):
    return 0.8 - 0.6 * math.exp(-0.3 * layer)


def kernel(x, positions, norm_mix, norm_ffn, norm_final, a_w_qkv, a_w_o, a_lambda, a_subln,
           b_w_o, c_w_qkv, c_w_o, router_group, router_expert, w_gate, w_up, w_down):
    x = x.reshape(TOKENS, D_MODEL)
    q_scale = HEAD_DIM ** -0.5
    rope = _rope_tables(positions, [q_scale, 1.0, None])
    final_gain = norm_final.reshape(1, D_MODEL)

    h = _rms_norm(x, norm_mix, 0, BF16)
    for i in range(DEPTH):
        kind, j = i % N_MIXERS, i // N_MIXERS
        if kind == 0:
            qkv = _matmul(h, a_w_qkv, j, n_out=3 * A_WIDTH, out_dtype=BF16, rope=rope)
            mixed = _diff_attention(qkv, a_lambda, a_subln, j, _lambda_init(i))
            x = _matmul(mixed, a_w_o, j, n_out=D_MODEL, out_dtype=F32, residual=x)
        elif kind == 1:
            mixed = _fourier_mix(h)
            x = _matmul(mixed, b_w_o, j, n_out=D_MODEL, out_dtype=F32, residual=x)
        else:
            qkv = _matmul(h, c_w_qkv, j, n_out=3 * C_GROUPS * C_WIDTH, out_dtype=BF16, rope=rope)
            parts = [_dilated_group(qkv, g) for g in range(C_GROUPS)]
            mixed = _merge_groups([p[0] for p in parts], [p[1] for p in parts])
            x = _matmul(mixed, c_w_o, j, n_out=D_MODEL, out_dtype=F32, residual=x)

        hf = _rms_norm(x, norm_ffn, i, F32)
        ids, gates, counts = _router(hf, router_group, router_expert, i)
        block_exp, n_used, slot_dst = _moe_plan(ids, counts)
        y = _experts(hf, block_exp, n_used, slot_dst, w_gate, w_up, w_down, i)
        if i + 1 < DEPTH:
            x, h = _moe_combine_norm(x, y, gates, norm_mix, i + 1, BF16, emit_sum=True)
        else:
            (out,) = _moe_combine_norm(x, y, gates, final_gain, 0, F32, emit_sum=False)
    return out.reshape(BATCH, SEQ, D_MODEL)
```

```python
import functools
import math

import numpy as np
import jax
import jax.numpy as jnp
from jax import lax
from jax.experimental import pallas as pl
from jax.experimental.pallas import tpu as pltpu

D_MODEL = 2048
BATCH = 4
SEQ = 2048
DEPTH = 4
TOKENS = BATCH * SEQ
N_MIXERS = 3
HEAD_DIM = 128
ROT_DIM = HEAD_DIM // 4
ROPE_THETA = 500000.0
A_HEADS = D_MODEL // (2 * HEAD_DIM)
A_WIDTH = A_HEADS * 2 * HEAD_DIM
FNET_GROUPS = 4
FNET_GROUP_WIDTH = D_MODEL // FNET_GROUPS
DILATIONS = (1, 4, 16)
DIL_HALF = 64
C_GROUPS = len(DILATIONS)
C_HEADS = D_MODEL // (2 * HEAD_DIM)
C_WIDTH = C_HEADS * HEAD_DIM
N_GROUPS = 8
EXPERTS_PER_GROUP = 8
N_EXPERTS = N_GROUPS * EXPERTS_PER_GROUP
TOP_K = 2
D_EXPERT = D_MODEL // 4
MOE_BLOCK = 128
MOE_BLOCKS = (TOKENS * TOP_K) // MOE_BLOCK + N_EXPERTS
MOE_SLOTS = MOE_BLOCKS * MOE_BLOCK
RMS_EPS = 1e-6
NEG_INF = -1e30
LANES = 128
VMEM_LIMIT = 52 * 1024 * 1024

F32 = jnp.float32
BF16 = jnp.bfloat16


def _params(*semantics):
    return pltpu.CompilerParams(dimension_semantics=semantics, vmem_limit_bytes=VMEM_LIMIT)


def _norm_kernel(x_ref, g_ref, h_ref):
    x = x_ref[...]
    ms = jnp.mean(x * x, axis=-1, keepdims=True)
    h_ref[...] = ((x * lax.rsqrt(ms + RMS_EPS)) * g_ref[...]).astype(h_ref.dtype)


def _rms_norm(x, gains, layer, out_dtype, tm=256):
    g3 = gains.reshape(gains.shape[0], 1, D_MODEL)
    return pl.pallas_call(
        _norm_kernel,
        grid=(TOKENS // tm,),
        in_specs=[pl.BlockSpec((tm, D_MODEL), lambda i: (i, 0)),
                  pl.BlockSpec((None, 1, D_MODEL), lambda i: (layer, 0, 0))],
        out_specs=pl.BlockSpec((tm, D_MODEL), lambda i: (i, 0)),
        out_shape=jax.ShapeDtypeStruct((TOKENS, D_MODEL), out_dtype),
        compiler_params=_params("arbitrary"),
        name="rms_norm",
    )(x, g3)


def _combine_kernel(x_ref, y0_ref, y1_ref, gate_ref, g_ref, *out_refs, emit_sum):
    gates = gate_ref[...]
    x = x_ref[...] + (gates[:, 0:1] * y0_ref[...] + gates[:, 1:2] * y1_ref[...])
    if emit_sum:
        out_refs[0][...] = x
    h_ref = out_refs[-1]
    ms = jnp.mean(x * x, axis=-1, keepdims=True)
    h_ref[...] = ((x * lax.rsqrt(ms + RMS_EPS)) * g_ref[...]).astype(h_ref.dtype)


def _moe_combine_norm(x, y, gates, gains, layer, out_dtype, emit_sum, tm=256):
    g3 = gains.reshape(-1, 1, D_MODEL)
    nt = TOKENS // tm
    row = pl.BlockSpec((tm, D_MODEL), lambda i: (i, 0))
    out_shape = [jax.ShapeDtypeStruct((TOKENS, D_MODEL), out_dtype)]
    out_specs = [row]
    if emit_sum:
        out_shape.insert(0, jax.ShapeDtypeStruct((TOKENS, D_MODEL), F32))
        out_specs.insert(0, row)
    return pl.pallas_call(
        functools.partial(_combine_kernel, emit_sum=emit_sum),
        grid=(nt,),
        in_specs=[row, row,
                  pl.BlockSpec((tm, D_MODEL), lambda i: (i + nt, 0)),
                  pl.BlockSpec((tm, LANES), lambda i: (i, 0)),
                  pl.BlockSpec((None, 1, D_MODEL), lambda i: (layer, 0, 0))],
        out_specs=out_specs,
        out_shape=out_shape,
        compiler_params=_params("arbitrary"),
        name="moe_combine_norm",
    )(x, y, y, gates, g3)


def _mm_kernel(*refs, rope, residual, tn, dil):
    a_ref, w_ref = refs[0], refs[1]
    pos = 2
    if rope:
        c_ref, s1_ref, s2_ref = refs[2:5]
        pos = 5
    if residual:
        r_ref = refs[pos]
        pos += 1
    o_ref, wbf_ref = refs[pos], refs[pos + 1]

    @pl.when(pl.program_id(1) == 0)
    def _():
        wbf_ref[...] = w_ref[...].astype(BF16)

    acc = jnp.dot(a_ref[...], wbf_ref[...], preferred_element_type=F32)
    if rope:
        reps = tn // LANES
        c = jnp.tile(c_ref[...], (1, reps))
        s1 = jnp.tile(s1_ref[...], (1, reps))
        s2 = jnp.tile(s2_ref[...], (1, reps))
        half = ROT_DIM // 2
        acc = acc * c + pltpu.roll(acc, tn - half, 1) * s1 + pltpu.roll(acc, half, 1) * s2
    if residual:
        acc = r_ref[...] + acc
    if dil == 1:
        o_ref[...] = acc.astype(o_ref.dtype)
    else:
        acc_ref = refs[pos + 2]
        for c in range(tn // LANES):
            acc_ref[c] = acc[:, c * LANES:(c + 1) * LANES]
        for r in range(dil):
            for c in range(tn // LANES):
                o_ref[r, :, c * LANES:(c + 1) * LANES] = (
                    acc_ref[c, pl.ds(r, o_ref.shape[1], stride=dil), :].astype(o_ref.dtype))


def _matmul(a, w, layer, *, n_out, out_dtype, tm=512, tn=512, residual=None, rope=None,
            w_col_block=None, dil=1):
    m, k = a.shape
    nj, ni = n_out // tn, m // tm
    if w_col_block is None:
        w_col_block = lambda j: j
    in_specs = [pl.BlockSpec((tm, k), lambda j, i: (i, 0)),
                pl.BlockSpec((None, k, tn), lambda j, i: (layer, 0, w_col_block(j)))]
    args = [a, w]
    if rope is not None:
        per_part = nj // rope[0].shape[0]
        tab = pl.BlockSpec((None, tm, LANES), lambda j, i: (j // per_part, i, 0))
        in_specs += [tab, tab, tab]
        args += list(rope)
    if residual is not None:
        in_specs.append(pl.BlockSpec((tm, tn), lambda j, i: (i, j)))
        args.append(residual)
    scratch = [pltpu.VMEM((k, tn), BF16)]
    if dil == 1:
        out_spec = pl.BlockSpec((tm, tn), lambda j, i: (i, j))
        out_shape = jax.ShapeDtypeStruct((m, n_out), out_dtype)
    else:
        out_spec = pl.BlockSpec((dil, tm // dil, tn), lambda j, i: (0, i, j))
        out_shape = jax.ShapeDtypeStruct((dil, m // dil, n_out), out_dtype)
        scratch.append(pltpu.VMEM((tn // LANES, tm, LANES), F32))
    return pl.pallas_call(
        functools.partial(_mm_kernel, rope=rope is not None, residual=residual is not None,
                          tn=tn, dil=dil),
        grid=(nj, ni),
        in_specs=in_specs,
        out_specs=out_spec,
        out_shape=out_shape,
        scratch_shapes=scratch,
        compiler_params=_params("arbitrary", "arbitrary"),
        name="matmul",
    )(*args)


def _rope_tables(positions, parts):
    inv_freq = ROPE_THETA ** (-jnp.arange(0, ROT_DIM, 2, dtype=F32) / ROT_DIM)
    ang = positions.astype(F32)[..., None] * inv_freq
    cos = jnp.cos(ang).reshape(TOKENS, ROT_DIM // 2)
    sin = jnp.sin(ang).reshape(TOKENS, ROT_DIM // 2)
    half = ROT_DIM // 2
    ones = jnp.ones((TOKENS, LANES - ROT_DIM), F32)
    c_full = jnp.concatenate([cos, cos, ones], axis=1)
    s_lo = jnp.concatenate([-sin, jnp.zeros((TOKENS, LANES - half), F32)], axis=1)
    s_hi = jnp.concatenate([jnp.zeros((TOKENS, half), F32), sin,
                            jnp.zeros((TOKENS, LANES - ROT_DIM), F32)], axis=1)
    cs, lo, hi = [], [], []
    for scale in parts:
        if scale is None:
            cs.append(jnp.ones((TOKENS, LANES), F32))
            lo.append(jnp.zeros((TOKENS, LANES), F32))
            hi.append(jnp.zeros((TOKENS, LANES), F32))
        else:
            cs.append(c_full * scale)
            lo.append(s_lo * scale)
            hi.append(s_hi * scale)
    return jnp.stack(cs), jnp.stack(lo), jnp.stack(hi)


def _softmax_pv(q, k, v):
    s = lax.dot_general(q, k, (((1,), (1,)), ((), ())), preferred_element_type=F32)
    m = jnp.max(s, axis=-1, keepdims=True)
    e = jnp.exp2(s - m)
    l = jnp.sum(e, axis=-1, keepdims=True)
    return jnp.dot(e.astype(BF16), v, preferred_element_type=F32) / l


def _diff_attn_kernel(q_ref, k_ref, v_ref, lam_ref, g_ref, o_ref, *, lam_init):
    lp = lam_ref[...]
    lam = (jnp.exp(jnp.sum(lp[0:1] * lp[1:2], axis=-1, keepdims=True))
           - jnp.exp(jnp.sum(lp[2:3] * lp[3:4], axis=-1, keepdims=True)) + lam_init)
    v = v_ref[...]
    o0 = _softmax_pv(q_ref[:, :HEAD_DIM], k_ref[:, :HEAD_DIM], v)
    o1 = _softmax_pv(q_ref[:, HEAD_DIM:], k_ref[:, HEAD_DIM:], v)
    out = o0 - lam * o1
    ms = jnp.mean(out * out, axis=-1, keepdims=True)
    y = (out * lax.rsqrt(ms + RMS_EPS)) * g_ref[...]
    o_ref[...] = (y * (1.0 - lam_init)).astype(o_ref.dtype)


def _diff_attention(qkv, lam_p, subln, j, lam_init, tq=256):
    nq = SEQ // tq
    hw = 2 * HEAD_DIM
    g3 = subln.reshape(-1, 1, hw)
    return pl.pallas_call(
        functools.partial(_diff_attn_kernel, lam_init=lam_init),
        grid=(BATCH, A_HEADS, nq),
        in_specs=[pl.BlockSpec((tq, hw), lambda b, h, t: (b * nq + t, h)),
                  pl.BlockSpec((SEQ, hw), lambda b, h, t: (b, A_HEADS + h)),
                  pl.BlockSpec((SEQ, hw), lambda b, h, t: (b, 2 * A_HEADS + h)),
                  pl.BlockSpec((None, 4, HEAD_DIM), lambda b, h, t: (j, 0, 0)),
                  pl.BlockSpec((None, 1, hw), lambda b, h, t: (j, 0, 0))],
        out_specs=pl.BlockSpec((tq, hw), lambda b, h, t: (b * nq + t, h)),
        out_shape=jax.ShapeDtypeStruct((TOKENS, A_WIDTH), BF16),
        compiler_params=_params("arbitrary", "arbitrary", "arbitrary"),
        name="diff_attention",
    )(qkv, qkv, qkv, lam_p, g3)


def _dft_tables():
    gw = FNET_GROUP_WIDTH
    ang_c = 2.0 * np.pi * (np.outer(np.arange(gw), np.arange(gw)) % gw) / gw
    scale = 1.0 / math.sqrt(SEQ * gw)
    chan = np.concatenate([np.cos(ang_c), np.sin(ang_c)], axis=1) * scale
    ang_s = 2.0 * np.pi * (np.outer(np.arange(SEQ), np.arange(SEQ)) % SEQ) / SEQ
    return (chan.astype(np.float32)[None], np.cos(ang_s).astype(np.float32),
            np.sin(ang_s).astype(np.float32))


def _seq_dft_kernel(cs_ref, ss_ref, u_ref, o_ref):
    gw = FNET_GROUP_WIDTH
    acc = (jnp.dot(cs_ref[...], u_ref[:, :gw], preferred_element_type=F32)
           - jnp.dot(ss_ref[...], u_ref[:, gw:], preferred_element_type=F32))
    o_ref[...] = acc.astype(o_ref.dtype)


def _chan_dft_kernel(h_ref, w_ref, u_ref):
    u_ref[...] = jnp.dot(h_ref[...], w_ref[...], preferred_element_type=F32).astype(u_ref.dtype)


def _fourier_mix(h, tr=512, tm=1024):
    gw = FNET_GROUP_WIDTH
    chan, cos_s, sin_s = _dft_tables()
    u = pl.pallas_call(
        _chan_dft_kernel,
        grid=(TOKENS // tm, FNET_GROUPS),
        in_specs=[pl.BlockSpec((tm, gw), lambda i, g: (i, g)),
                  pl.BlockSpec((gw, 2 * gw), lambda i, g: (0, 0))],
        out_specs=pl.BlockSpec((tm, 2 * gw), lambda i, g: (i, g)),
        out_shape=jax.ShapeDtypeStruct((TOKENS, FNET_GROUPS * 2 * gw), BF16),
        compiler_params=_params("arbitrary", "arbitrary"),
        name="chan_dft",
    )(h, jnp.asarray(chan[0]).astype(BF16))
    nt = SEQ // tr
    return pl.pallas_call(
        _seq_dft_kernel,
        grid=(BATCH, FNET_GROUPS, nt),
        in_specs=[pl.BlockSpec((tr, SEQ), lambda b, g, t: (t, 0)),
                  pl.BlockSpec((tr, SEQ), lambda b, g, t: (t, 0)),
                  pl.BlockSpec((SEQ, 2 * gw), lambda b, g, t: (b, g))],
        out_specs=pl.BlockSpec((tr, gw), lambda b, g, t: (b * nt + t, g)),
        out_shape=jax.ShapeDtypeStruct((TOKENS, D_MODEL), BF16),
        compiler_params=_params("arbitrary", "arbitrary", "arbitrary"),
        name="seq_dft",
    )(jnp.asarray(cos_s).astype(BF16), jnp.asarray(sin_s).astype(BF16), u)


def _dil_attn_kernel(q_ref, k_ref, v_ref, o_ref, lse_ref, *, seq_len, tq, win):
    m0 = pl.program_id(2) * tq
    start = pl.multiple_of(jnp.clip(m0 - (win - tq) // 2, 0, seq_len - win), tq)
    qpos = m0 + lax.broadcasted_iota(jnp.int32, (tq, win), 0)
    kpos = start + lax.broadcasted_iota(jnp.int32, (tq, win), 1)
    valid = jnp.abs(kpos - qpos) <= DIL_HALF
    lane = lax.broadcasted_iota(jnp.int32, (tq, LANES), 1)
    lse_all = jnp.zeros((tq, LANES), F32)
    for h in range(C_HEADS):
        cols = slice(h * HEAD_DIM, (h + 1) * HEAD_DIM)
        s = lax.dot_general(q_ref[:, cols], k_ref[pl.ds(start, win), cols],
                            (((1,), (1,)), ((), ())), preferred_element_type=F32)
        s = jnp.where(valid, s, NEG_INF)
        m = jnp.max(s, axis=-1, keepdims=True)
        e = jnp.exp2(s - m)
        l = jnp.sum(e, axis=-1, keepdims=True)
        o = jnp.dot(e.astype(BF16), v_ref[pl.ds(start, win), cols], preferred_element_type=F32)
        o_ref[:, cols] = o / l
        lse_all = jnp.where(lane == h, (m + jnp.log2(l)) * math.log(2.0), lse_all)
    lse_ref[...] = lse_all


def _dilated_group(qkv, g, tq=128):
    dil = DILATIONS[g]
    seq_len = SEQ // dil
    win = min(3 * tq, seq_len)
    nq = seq_len // tq
    o, lse = pl.pallas_call(
        functools.partial(_dil_attn_kernel, seq_len=seq_len, tq=tq, win=win),
        grid=(BATCH, dil, nq),
        in_specs=[pl.BlockSpec((None, tq, C_WIDTH), lambda b, r, t: (r, b * nq + t, 0)),
                  pl.BlockSpec((None, seq_len, C_WIDTH), lambda b, r, t: (r, b, 1)),
                  pl.BlockSpec((None, seq_len, C_WIDTH), lambda b, r, t: (r, b, 2))],
        out_specs=[pl.BlockSpec((tq, C_WIDTH), lambda b, r, t: (b * nq + t, r)),
                   pl.BlockSpec((tq, LANES), lambda b, r, t: (b * nq + t, r))],
        out_shape=[jax.ShapeDtypeStruct((BATCH * seq_len, dil * C_WIDTH), F32),
                   jax.ShapeDtypeStruct((BATCH * seq_len, dil * LANES), F32)],
        compiler_params=_params("arbitrary", "arbitrary", "arbitrary"),
        name="dilated_attention",
    )(qkv, qkv, qkv)
    return o.reshape(TOKENS, C_WIDTH), lse.reshape(TOKENS, LANES)


def _merge_kernel(o0_ref, o1_ref, o2_ref, l0_ref, l1_ref, l2_ref, out_ref):
    l0, l1, l2 = l0_ref[...], l1_ref[...], l2_ref[...]
    mx = jnp.maximum(jnp.maximum(l0, l1), l2)
    w0, w1, w2 = jnp.exp(l0 - mx), jnp.exp(l1 - mx), jnp.exp(l2 - mx)
    den = w0 + w1 + w2
    a0, a1, a2 = w0 / den, w1 / den, w2 / den
    for h in range(C_HEADS):
        cols = slice(h * HEAD_DIM, (h + 1) * HEAD_DIM)
        merged = (a0[:, h:h + 1] * o0_ref[:, cols] + a1[:, h:h + 1] * o1_ref[:, cols]
                  + a2[:, h:h + 1] * o2_ref[:, cols])
        out_ref[:, cols] = merged.astype(out_ref.dtype)


def _merge_groups(outs, lses, tm=256):
    row = pl.BlockSpec((tm, C_WIDTH), lambda i: (i, 0))
    lrow = pl.BlockSpec((tm, LANES), lambda i: (i, 0))
    return pl.pallas_call(
        _merge_kernel,
        grid=(TOKENS // tm,),
        in_specs=[row, row, row, lrow, lrow, lrow],
        out_specs=row,
        out_shape=jax.ShapeDtypeStruct((TOKENS, C_WIDTH), BF16),
        compiler_params=_params("arbitrary"),
        name="merge_dilation_groups",
    )(*outs, *lses)


def _router_kernel(x_ref, g_ref, wr_ref, h_ref, ids_ref, gates_ref, cnt_ref, carry_ref, *, tm):
    @pl.when(pl.program_id(0) == 0)
    def _():
        carry_ref[...] = jnp.zeros_like(carry_ref)

    x = x_ref[...]
    ms = jnp.mean(x * x, axis=-1, keepdims=True)
    h = (x * lax.rsqrt(ms + RMS_EPS)) * g_ref[...]
    h_ref[...] = h
    logits = jnp.dot(h, wr_ref[...], preferred_element_type=F32,
                     precision=lax.Precision.HIGHEST)
    lane = lax.broadcasted_iota(jnp.int32, (tm, LANES), 1).astype(F32)
    neg = jnp.float32(-jnp.inf)
    big = jnp.float32(4 * LANES)
    gl = jnp.where(lane < N_GROUPS, logits, neg)
    gmax = jnp.max(gl, axis=-1, keepdims=True)
    g_sel = jnp.min(jnp.where(gl == gmax, lane, big), axis=-1, keepdims=True)
    g_prob = 1.0 / jnp.sum(jnp.exp(gl - gmax), axis=-1, keepdims=True)
    lo = N_GROUPS + g_sel * EXPERTS_PER_GROUP
    el = jnp.where((lane >= lo) & (lane < lo + EXPERTS_PER_GROUP), logits, neg)
    v1 = jnp.max(el, axis=-1, keepdims=True)
    i1 = jnp.min(jnp.where(el == v1, lane, big), axis=-1, keepdims=True)
    el2 = jnp.where(lane == i1, neg, el)
    v2 = jnp.max(el2, axis=-1, keepdims=True)
    i2 = jnp.min(jnp.where(el2 == v2, lane, big), axis=-1, keepdims=True)
    t = jnp.exp(v2 - v1)
    gate1 = g_prob * (1.0 / (1.0 + t))
    gate2 = g_prob * (t / (1.0 + t))
    oh1 = lane == i1
    oh2 = lane == i2
    both = jnp.where(oh1 | oh2, 1.0, 0.0)
    row = lax.broadcasted_iota(jnp.int32, (tm, tm), 0)
    col = lax.broadcasted_iota(jnp.int32, (tm, tm), 1)
    lower = jnp.where(col < row, 1.0, 0.0).astype(BF16)
    carry = carry_ref[0:1, :]
    cum = jnp.dot(lower, both.astype(BF16), preferred_element_type=F32) + carry
    r1 = jnp.sum(jnp.where(oh1, cum, 0.0), axis=-1, keepdims=True)
    r2 = jnp.sum(jnp.where(oh2, cum, 0.0), axis=-1, keepdims=True)
    total = carry + jnp.sum(both, axis=0, keepdims=True)
    carry_ref[...] = jnp.broadcast_to(total, carry_ref.shape)
    cnt_ref[...] = jnp.broadcast_to(total, cnt_ref.shape)
    ids = jnp.where(lane == 0, i1 - N_GROUPS,
                    jnp.where(lane == 1, i2 - N_GROUPS,
                              jnp.where(lane == 2, r1, jnp.where(lane == 3, r2, 0.0))))
    ids_ref[...] = ids.astype(jnp.int32)
    gates_ref[...] = jnp.where(lane == 0, gate1, jnp.where(lane == 1, gate2, 0.0))


def _router(x, gains, w_rg, w_re, layer, tm=512):
    wr = jnp.concatenate([w_rg[layer], w_re[layer],
                          jnp.zeros((D_MODEL, LANES - N_GROUPS - N_EXPERTS), F32)], axis=1)
    g3 = gains.reshape(-1, 1, D_MODEL)
    wide = pl.BlockSpec((tm, D_MODEL), lambda i: (i, 0))
    row = pl.BlockSpec((tm, LANES), lambda i: (i, 0))
    return pl.pallas_call(
        functools.partial(_router_kernel, tm=tm),
        grid=(TOKENS // tm,),
        in_specs=[wide,
                  pl.BlockSpec((None, 1, D_MODEL), lambda i: (layer, 0, 0)),
                  pl.BlockSpec((D_MODEL, LANES), lambda i: (0, 0))],
        out_specs=[wide, row, row, pl.BlockSpec((8, LANES), lambda i: (0, 0))],
        out_shape=[jax.ShapeDtypeStruct((TOKENS, D_MODEL), F32),
                   jax.ShapeDtypeStruct((TOKENS, LANES), jnp.int32),
                   jax.ShapeDtypeStruct((TOKENS, LANES), F32),
                   jax.ShapeDtypeStruct((8, LANES), F32)],
        scratch_shapes=[pltpu.VMEM((8, LANES), F32)],
        compiler_params=_params("arbitrary"),
        name="moe_router",
    )(x, g3, wr)


def _expert_kernel(be_ref, nu_ref, dst_ref, h_hbm, wg_ref, wu_ref, wd_ref, y_hbm,
                   xbuf, ybuf, wg_bf, wu_bf, wd_bf, gsem, ssem):
    i = pl.program_id(0)
    n_used = nu_ref[0]
    slot = lax.rem(i, 2)

    def gather_copy(blk, r, sl):
        tok = dst_ref[blk * MOE_BLOCK + r] & (TOKENS - 1)
        return pltpu.make_async_copy(h_hbm.at[pl.ds(tok, 1)], xbuf.at[sl, pl.ds(r, 1)], gsem.at[sl])

    def scatter_copy(blk, r):
        dst = dst_ref[blk * MOE_BLOCK + r]
        return pltpu.make_async_copy(ybuf.at[pl.ds(r, 1)], y_hbm.at[pl.ds(dst, 1)], ssem.at[0])

    def for_rows(fn):
        def body(r, carry):
            fn(r)
            return carry
        lax.fori_loop(0, MOE_BLOCK, body, 0)

    @pl.when(i == 0)
    def _():
        ybuf[...] = jnp.zeros_like(ybuf)
        spare = pltpu.make_async_copy(ybuf, y_hbm.at[pl.ds(TOP_K * TOKENS, MOE_BLOCK)], ssem.at[0])
        spare.start()
        spare.wait()

    @pl.when((i == 0) & (n_used > 0))
    def _():
        for_rows(lambda r: gather_copy(0, r, 0).start())

    @pl.when(i + 1 < n_used)
    def _():
        for_rows(lambda r: gather_copy(i + 1, r, 1 - slot).start())

    @pl.when((i < n_used) & ((i == 0) | (be_ref[i] != be_ref[jnp.maximum(i - 1, 0)])))
    def _():
        wg_bf[...] = wg_ref[...].astype(BF16)
        wu_bf[...] = wu_ref[...].astype(BF16)
        wd_bf[...] = wd_ref[...].astype(BF16)

    @pl.when((i > 0) & (i <= n_used))
    def _():
        for_rows(lambda r: scatter_copy(i - 1, r).wait())

    @pl.when(i < n_used)
    def _():
        for_rows(lambda r: gather_copy(i, r, slot).wait())
        x = xbuf[slot].astype(BF16)
        gate = jnp.dot(x, wg_bf[...], preferred_element_type=F32)
        up = jnp.dot(x, wu_bf[...], preferred_element_type=F32)
        hid = (gate * jax.nn.sigmoid(gate)) * up
        ybuf[...] = jnp.dot(hid.astype(BF16), wd_bf[...], preferred_element_type=F32)
        for_rows(lambda r: scatter_copy(i, r).start())

    @pl.when((i == MOE_BLOCKS - 1) & (i < n_used))
    def _():
        for_rows(lambda r: scatter_copy(i, r).wait())


def _experts(hf, block_exp, n_used, slot_dst, w_gate, w_up, w_down, layer):
    grid_spec = pltpu.PrefetchScalarGridSpec(
        num_scalar_prefetch=3,
        grid=(MOE_BLOCKS,),
        in_specs=[pl.BlockSpec(memory_space=pl.ANY),
                  pl.BlockSpec((None, None, D_MODEL, D_EXPERT), lambda i, be, nu, sd: (layer, be[i], 0, 0)),
                  pl.BlockSpec((None, None, D_MODEL, D_EXPERT), lambda i, be, nu, sd: (layer, be[i], 0, 0)),
                  pl.BlockSpec((None, None, D_EXPERT, D_MODEL), lambda i, be, nu, sd: (layer, be[i], 0, 0))],
        out_specs=pl.BlockSpec(memory_space=pl.ANY),
        scratch_shapes=[pltpu.VMEM((2, MOE_BLOCK, D_MODEL), F32),
                        pltpu.VMEM((MOE_BLOCK, D_MODEL), F32),
                        pltpu.VMEM((D_MODEL, D_EXPERT), BF16),
                        pltpu.VMEM((D_MODEL, D_EXPERT), BF16),
                        pltpu.VMEM((D_EXPERT, D_MODEL), BF16),
                        pltpu.SemaphoreType.DMA((2,)),
                        pltpu.SemaphoreType.DMA((1,))])
    return pl.pallas_call(
        _expert_kernel,
        grid_spec=grid_spec,
        out_shape=jax.ShapeDtypeStruct((TOP_K * TOKENS + MOE_BLOCK, D_MODEL), F32),
        compiler_params=_params("arbitrary"),
        name="moe_experts",
    )(block_exp, n_used, slot_dst, hf, w_gate, w_up, w_down)


def _moe_plan(ids, counts):
    expert = ids[:, 0:TOP_K]
    rank = ids[:, TOP_K:2 * TOP_K]
    cnt = counts[0, N_GROUPS:N_GROUPS + N_EXPERTS].astype(jnp.int32)
    n_blk = (cnt + MOE_BLOCK - 1) // MOE_BLOCK
    block_start = jnp.concatenate([jnp.zeros((1,), jnp.int32), jnp.cumsum(n_blk).astype(jnp.int32)])
    row_start = block_start[:N_EXPERTS] * MOE_BLOCK
    hit = expert[..., None] == jnp.arange(N_EXPERTS, dtype=jnp.int32)
    dest = jnp.sum(jnp.where(hit, row_start, 0), axis=-1) + rank
    n_used = block_start[N_EXPERTS]
    blk = jnp.arange(MOE_BLOCKS, dtype=jnp.int32)
    block_exp = jnp.sum((block_start[1:][None, :] <= jnp.minimum(blk, n_used - 1)[:, None]).astype(jnp.int32), axis=1)
    block_exp = jnp.minimum(block_exp, N_EXPERTS - 1)
    spare = TOP_K * TOKENS + (jnp.arange(MOE_SLOTS, dtype=jnp.int32) % MOE_BLOCK)
    slot_dst = spare.at[dest.T.reshape(-1)].set(jnp.arange(TOP_K * TOKENS, dtype=jnp.int32))
    return block_exp, n_used.reshape(1), slot_dst


def _lambda_init(layer):
    return 0.8 - 0.6 * math.exp(-0.3 * layer)


def kernel(x, positions, norm_mix, norm_ffn, norm_final, a_w_qkv, a_w_o, a_lambda, a_subln,
           b_w_o, c_w_qkv, c_w_o, router_group, router_expert, w_gate, w_up, w_down):
    x = x.reshape(TOKENS, D_MODEL)
    q_scale = HEAD_DIM ** -0.5 * math.log2(math.e)
    rope = _rope_tables(positions, [q_scale, 1.0, None])
    final_gain = norm_final.reshape(1, D_MODEL)

    h = _rms_norm(x, norm_mix, 0, BF16)
    for i in range(DEPTH):
        kind, j = i % N_MIXERS, i // N_MIXERS
        if kind == 0:
            qkv = _matmul(h, a_w_qkv, j, n_out=3 * A_WIDTH, out_dtype=BF16, rope=rope)
            mixed = _diff_attention(qkv, a_lambda, a_subln, j, _lambda_init(i))
            x = _matmul(mixed, a_w_o, j, n_out=D_MODEL, out_dtype=F32, residual=x)
        elif kind == 1:
            mixed = _fourier_mix(h)
            x = _matmul(mixed, b_w_o, j, n_out=D_MODEL, out_dtype=F32, residual=x)
        else:
            parts = []
            for g, dil in enumerate(DILATIONS):
                per = C_WIDTH // 512
                qkv = _matmul(h, c_w_qkv, j, n_out=3 * C_WIDTH, out_dtype=BF16, rope=rope, dil=dil,
                              w_col_block=lambda t, g=g, per=per: ((t // per) * C_GROUPS + g) * per + t % per)
                parts.append(_dilated_group(qkv.reshape(dil, TOKENS // dil, 3 * C_WIDTH), g))
            mixed = _merge_groups([p[0] for p in parts], [p[1] for p in parts])
            x = _matmul(mixed, c_w_o, j, n_out=D_MODEL, out_dtype=F32, residual=x)

        hf, ids, gates, counts = _router(x, norm_ffn, router_group, router_expert, i)
        block_exp, n_used, slot_dst = _moe_plan(ids, counts)
        y = _experts(hf, block_exp, n_used, slot_dst, w_gate, w_up, w_down, i)
        if i + 1 < DEPTH:
            x, h = _moe_combine_norm(x, y, gates, norm_mix, i + 1, BF16, emit_sum=True)
        else:
            (out,) = _moe_combine_norm(x, y, gates, final_gain, 0, F32, emit_sum=False)
    return out.reshape(BATCH, SEQ, D_MODEL)
```
